```python
import math
import jax, jax.numpy as jnp
from jax import lax
import numpy as np

D_MODEL = 1024
BATCH = 32
SEQ = 2048
DEPTH = 1

CHUNK = 64
Q_BLOCK = 128
EPS = 1e-6
CONV_CH = D_MODEL
CONV_WIDTH = 31
MLA_HEADS = 8
QK_NOPE = 128
QK_ROPE = 64
V_DIM = 128
Q_LORA = D_MODEL // 4
KV_LORA = D_MODEL // 4
ROPE_THETA = 10000.0
PEER_HEADS = 8
PEER_KEYS = 128
PEER_EXPERTS = PEER_KEYS * PEER_KEYS
PEER_QDIM = 256
PEER_TOPK = 16
TOKEN_BLOCK = 128
IN_SIZES = (CONV_CH, CONV_CH, Q_LORA, KV_LORA, QK_ROPE, D_MODEL, D_MODEL)
IN_COLS = sum(IN_SIZES)
IN_SPLITS = [int(i) for i in np.cumsum(IN_SIZES)[:-1]]

kernel_name = "hybrid_conformer_mla_peer_block"


def rms_norm(x, g):
    xf = x.astype(jnp.float32)
    y = xf * lax.rsqrt(jnp.mean(xf * xf, axis=-1, keepdims=True) + EPS)
    return (y * g.astype(jnp.float32)).astype(x.dtype)


def layer_norm(x, g, b):
    xf = x.astype(jnp.float32)
    mu = jnp.mean(xf, axis=-1, keepdims=True)
    var = jnp.mean(jnp.square(xf - mu), axis=-1, keepdims=True)
    y = (xf - mu) * lax.rsqrt(var + EPS)
    return (y * g.astype(jnp.float32) + b.astype(jnp.float32)).astype(x.dtype)


def rope_tables(seq, dim):
    pos = jnp.arange(seq, dtype=jnp.float32)
    inv_freq = ROPE_THETA ** (-jnp.arange(0, dim, 2, dtype=jnp.float32) / dim)
    ang = pos[:, None] * inv_freq[None, :]
    return jnp.cos(ang), jnp.sin(ang)


def apply_rope(x, cos, sin):
    half = x.shape[-1] // 2
    xf = x.astype(jnp.float32)
    x1, x2 = xf[..., :half], xf[..., half:]
    c = cos[None, :, None, :]
    s = sin[None, :, None, :]
    return jnp.concatenate([x1 * c - x2 * s, x1 * s + x2 * c], axis=-1).astype(x.dtype)


def conformer_conv_branch(a, b, dw, dw_b, ln_g, ln_b, w_pw):
    u = a * jax.nn.sigmoid(b)
    y = lax.conv_general_dilated(
        u, dw[:, None, :], window_strides=(1,), padding=[(CONV_WIDTH - 1, 0)],
        dimension_numbers=("NWC", "WIO", "NWC"), feature_group_count=CONV_CH) + dw_b
    y = jax.nn.silu(layer_norm(y, ln_g, ln_b))
    return y @ w_pw


def mla_branch(q_c, c_kv, k_rope, q_norm_g, w_q_up, kv_norm_g, w_kv_up, w_o):
    B, S, _ = q_c.shape
    q = (rms_norm(q_c, q_norm_g) @ w_q_up).reshape(B, S, MLA_HEADS, QK_NOPE + QK_ROPE)
    kv = (rms_norm(c_kv, kv_norm_g) @ w_kv_up).reshape(B, S, MLA_HEADS, QK_NOPE + V_DIM)
    cos, sin = rope_tables(S, QK_ROPE)
    q = jnp.concatenate([q[..., :QK_NOPE], apply_rope(q[..., QK_NOPE:], cos, sin)], axis=-1)
    kr = apply_rope(k_rope[:, :, None, :], cos, sin)
    k = jnp.concatenate([kv[..., :QK_NOPE], jnp.broadcast_to(kr, (B, S, MLA_HEADS, QK_ROPE))], axis=-1)
    v = kv[..., QK_NOPE:]
    scale = 1.0 / math.sqrt(QK_NOPE + QK_ROPE)
    chunk_id = jnp.arange(S) // CHUNK
    outs = []
    for i in range(S // Q_BLOCK):
        q0, q1 = i * Q_BLOCK, (i + 1) * Q_BLOCK
        s = jnp.einsum("bqhd,bkhd->bhqk", q[:, q0:q1], k[:, :q1]).astype(jnp.float32) * scale
        mask = chunk_id[q0:q1, None] >= chunk_id[None, :q1]
        p = jax.nn.softmax(jnp.where(mask, s, -jnp.inf), axis=-1).astype(v.dtype)
        outs.append(jnp.einsum("bhqk,bkhd->bqhd", p, v[:, :q1]))
    o = jnp.concatenate(outs, axis=1).reshape(B, S, MLA_HEADS * V_DIM)
    return o @ w_o


def peer_ffn(xn, w_q, sub_keys, u_tab, v_tab):
    B, S, D = xn.shape
    blocks = xn.reshape(B * S // TOKEN_BLOCK, TOKEN_BLOCK, D)
    n_cand = PEER_TOPK * PEER_TOPK

    def block_fn(xb):
        q = (xb @ w_q).reshape(TOKEN_BLOCK, PEER_HEADS, 2, PEER_QDIM // 2)
        sc = jnp.einsum("thcd,hckd->thck", q, sub_keys).astype(jnp.float32)
        s1, i1 = lax.top_k(sc[:, :, 0], PEER_TOPK)
        s2, i2 = lax.top_k(sc[:, :, 1], PEER_TOPK)
        cand = (s1[..., :, None] + s2[..., None, :]).reshape(TOKEN_BLOCK, PEER_HEADS, n_cand)
        cand_idx = (i1[..., :, None] * PEER_KEYS + i2[..., None, :]).reshape(TOKEN_BLOCK, PEER_HEADS, n_cand)
        top_s, pos = lax.top_k(cand, PEER_TOPK)
        idx = jnp.take_along_axis(cand_idx, pos, axis=-1)
        g = jax.nn.softmax(top_s, axis=-1).astype(xb.dtype)
        u_sel = jnp.take(u_tab, idx, axis=0)
        act = jax.nn.gelu(jnp.einsum("thkd,td->thk", u_sel, xb)) * g
        v_sel = jnp.take(v_tab, idx, axis=0)
        return jnp.einsum("thk,thkd->td", act, v_sel)

    return lax.map(block_fn, blocks).reshape(B, S, D)


def hybrid_layer(x, norm_mix_g, w_in, b_gate, conv_dw, conv_dw_b, conv_ln_g, conv_ln_b,
                 conv_w_pw, q_norm_g, w_q_up, kv_norm_g, w_kv_up, mla_w_o, w_out,
                 norm_ffn_g, peer_w_q, peer_sub_keys, peer_u, peer_v):
    n = rms_norm(x, norm_mix_g)
    z = n @ w_in
    glu_a, glu_b, q_c, c_kv, k_rope, gate_c, gate_m = jnp.split(z, IN_SPLITS, axis=-1)
    y_conv = conformer_conv_branch(glu_a, glu_b, conv_dw, conv_dw_b, conv_ln_g, conv_ln_b, conv_w_pw)
    y_mla = mla_branch(q_c, c_kv, k_rope, q_norm_g, w_q_up, kv_norm_g, w_kv_up, mla_w_o)
    merged = jax.nn.sigmoid(gate_c + b_gate[0]) * y_conv + jax.nn.sigmoid(gate_m + b_gate[1]) * y_mla
    h = x + merged @ w_out
    return h + peer_ffn(rms_norm(h, norm_ffn_g), peer_w_q, peer_sub_keys, peer_u, peer_v)


def setup_inputs(seed: int = 0) -> dict:
    key = jax.random.key(seed)
    ks = jax.random.split(key, 24)

    def nrm(k, shape, scale):
        return jax.random.normal(k, shape, dtype=jnp.float32) * scale

    def gain(k, shape):
        return 1.0 + nrm(k, shape, 0.02)

    L = DEPTH
    return {
        "x": nrm(ks[0], (BATCH, SEQ, D_MODEL), 1.0),
        "norm_mix_g": gain(ks[1], (L, D_MODEL)),
        "w_in": nrm(ks[2], (L, D_MODEL, IN_COLS), D_MODEL ** -0.5),
        "b_gate": nrm(ks[3], (L, 2, D_MODEL), 0.02),
        "conv_dw": nrm(ks[4], (L, CONV_WIDTH, CONV_CH), CONV_WIDTH ** -0.5),
        "conv_dw_b": nrm(ks[5], (L, CONV_CH), 0.02),
        "conv_ln_g": gain(ks[6], (L, CONV_CH)),
        "conv_ln_b": nrm(ks[7], (L, CONV_CH), 0.02),
        "conv_w_pw": nrm(ks[8], (L, CONV_CH, D_MODEL), CONV_CH ** -0.5),
        "q_norm_g": gain(ks[9], (L, Q_LORA)),
        "w_q_up": nrm(ks[10], (L, Q_LORA, MLA_HEADS * (QK_NOPE + QK_ROPE)), Q_LORA ** -0.5),
        "kv_norm_g": gain(ks[11], (L, KV_LORA)),
        "w_kv_up": nrm(ks[12], (L, KV_LORA, MLA_HEADS * (QK_NOPE + V_DIM)), KV_LORA ** -0.5),
        "mla_w_o": nrm(ks[13], (L, MLA_HEADS * V_DIM, D_MODEL), (MLA_HEADS * V_DIM) ** -0.5),
        "w_out": nrm(ks[14], (L, D_MODEL, D_MODEL), D_MODEL ** -0.5),
        "norm_ffn_g": gain(ks[15], (L, D_MODEL)),
        "peer_w_q": nrm(ks[16], (L, D_MODEL, PEER_HEADS * PEER_QDIM), D_MODEL ** -0.5),
        "peer_sub_keys": nrm(ks[17], (L, PEER_HEADS, 2, PEER_KEYS, PEER_QDIM // 2), (PEER_QDIM // 2) ** -0.5),
        "peer_u": nrm(ks[18], (L, PEER_EXPERTS, D_MODEL), D_MODEL ** -0.5),
        "peer_v": nrm(ks[19], (L, PEER_EXPERTS, D_MODEL), D_MODEL ** -0.5),
        "norm_final_g": gain(ks[20], (D_MODEL,)),
    }


def reference(x, norm_mix_g, w_in, b_gate, conv_dw, conv_dw_b, conv_ln_g, conv_ln_b,
              conv_w_pw, q_norm_g, w_q_up, kv_norm_g, w_kv_up, mla_w_o, w_out,
              norm_ffn_g, peer_w_q, peer_sub_keys, peer_u, peer_v, norm_final_g):
    h = x
    for l in range(DEPTH):
        h = hybrid_layer(h, norm_mix_g[l], w_in[l], b_gate[l], conv_dw[l], conv_dw_b[l],
                         conv_ln_g[l], conv_ln_b[l], conv_w_pw[l], q_norm_g[l], w_q_up[l],
                         kv_norm_g[l], w_kv_up[l], mla_w_o[l], w_out[l], norm_ffn_g[l],
                         peer_w_q[l], peer_sub_keys[l], peer_u[l], peer_v[l])
    return rms_norm(h, norm_final_g)
```

```python
import functools
import math

import jax
import jax.numpy as jnp
import numpy as np
from jax import lax
from jax.experimental import pallas as pl
from jax.experimental.pallas import tpu as pltpu

EPS = 1e-6
CHUNK = 64
CONV_WIDTH = 31
MLA_HEADS = 8
QK_NOPE = 128
QK_ROPE = 64
V_DIM = 128
ROPE_THETA = 10000.0
PEER_HEADS = 8
PEER_KEYS = 128
PEER_TOPK = 16
N_PAIR = PEER_HEADS * PEER_TOPK

LANES = 128
ROW_WORDS = 4
TILE_STRIDE = 136
VMEM_LIMIT = 56 * 1024 * 1024

F32 = jnp.float32
BF16 = jnp.bfloat16
NT_DIMS = (((1,), (1,)), ((), ()))


def _cparams(sem):
    return pltpu.CompilerParams(dimension_semantics=sem, vmem_limit_bytes=VMEM_LIMIT)


def _rms(x, g):
    return x * lax.rsqrt(jnp.mean(x * x, axis=-1, keepdims=True) + EPS) * g


def _resident(shape):
    nd = len(shape)
    return pl.BlockSpec(shape, lambda *_: (0,) * nd, pipeline_mode=pl.Buffered(1))


def _inproj_kernel(x_ref, gmix_ref, wa_ref, wb_ref, wq_ref, wkv_ref, wkr_ref, wkrs_ref, wgc_ref, wgm_ref,
                   bg_ref, qg_ref, kvg_ref, cos_ref, sin_ref,
                   u_ref, qn_ref, kvn_ref, kr_ref, sgc_ref, sgm_ref):
    xn = _rms(x_ref[...], gmix_ref[...]).astype(BF16)

    def proj(w_ref):
        return jnp.dot(xn, w_ref[...], preferred_element_type=F32)

    u_ref[...] = proj(wa_ref) * jax.nn.sigmoid(proj(wb_ref))
    qn_ref[...] = _rms(proj(wq_ref), qg_ref[...]).astype(BF16)
    kvn_ref[...] = _rms(proj(wkv_ref), kvg_ref[...]).astype(BF16)
    kr_ref[...] = (proj(wkr_ref) * cos_ref[...] + proj(wkrs_ref) * sin_ref[...]).astype(BF16)
    sgc_ref[...] = jax.nn.sigmoid(proj(wgc_ref) + bg_ref[0:1, :]).astype(BF16)
    sgm_ref[...] = jax.nn.sigmoid(proj(wgm_ref) + bg_ref[1:2, :]).astype(BF16)


def _in_proj(x, gmix, ws, bg, qg, kvg, cos2, sin2, seq, tm=256):
    n, d = x.shape
    pos_blocks = seq // tm
    row = lambda c: pl.BlockSpec((tm, c), lambda i: (i, 0))
    pos = pl.BlockSpec((tm, QK_ROPE), lambda i: (i % pos_blocks, 0))
    in_specs = [row(d), _resident(gmix.shape)] + [_resident(w.shape) for w in ws]
    in_specs += [_resident(bg.shape), _resident(qg.shape), _resident(kvg.shape), pos, pos]
    q_lora = qg.shape[1]
    out_shape = (
        jax.ShapeDtypeStruct((n, d), F32),
        jax.ShapeDtypeStruct((n, q_lora), BF16),
        jax.ShapeDtypeStruct((n, q_lora), BF16),
        jax.ShapeDtypeStruct((n, QK_ROPE), BF16),
        jax.ShapeDtypeStruct((n, d), BF16),
        jax.ShapeDtypeStruct((n, d), BF16),
    )
    out_specs = (row(d), row(q_lora), row(q_lora), row(QK_ROPE), row(d), row(d))
    return pl.pallas_call(
        _inproj_kernel, grid=(n // tm,), in_specs=in_specs, out_specs=out_specs, out_shape=out_shape,
        compiler_params=_cparams(("parallel",)), name="in_proj",
    )(x, gmix, *ws, bg, qg, kvg, cos2, sin2)


CONV_PAD = 32
CONV_HALO = 40


def _conv_kernel(u_ref, dw_ref, dwb_ref, lng_ref, lnb_ref, out_ref, pad_ref, *, seq, tt):
    ch = u_ref.shape[1]
    pad_ref[0:CONV_PAD, :] = jnp.zeros((CONV_PAD, ch), F32)
    pad_ref[CONV_PAD:CONV_PAD + seq, :] = u_ref[...]
    pad_ref[CONV_PAD + seq:CONV_PAD + seq + 8, :] = jnp.zeros((8, ch), F32)
    first = CONV_PAD - (CONV_WIDTH - 1)

    def tile(i, carry):
        t0 = pl.multiple_of(i * tt, tt)
        win = pad_ref[pl.ds(t0, tt + CONV_HALO), :]
        acc = jnp.broadcast_to(dwb_ref[...], (tt, ch))
        for s in range(8):
            taps = [w for w in range(CONV_WIDTH) if (first + w) % 8 == s]
            if not taps:
                continue
            shifted = win[s:s + tt + CONV_PAD, :]
            for w in taps:
                k = (first + w) // 8 * 8
                acc = acc + dw_ref[w:w + 1, :] * shifted[k:k + tt, :]
        mu = jnp.mean(acc, axis=-1, keepdims=True)
        cen = acc - mu
        var = jnp.mean(cen * cen, axis=-1, keepdims=True)
        y = cen * lax.rsqrt(var + EPS) * lng_ref[...] + lnb_ref[...]
        out_ref[pl.ds(t0, tt), :] = (y * jax.nn.sigmoid(y)).astype(BF16)
        return carry

    lax.fori_loop(0, seq // tt, tile, 0)


def _conv(u, dw, dwb, lng, lnb, seq, tt=32):
    n, ch = u.shape
    return pl.pallas_call(
        functools.partial(_conv_kernel, seq=seq, tt=tt),
        grid=(n // seq,),
        in_specs=[pl.BlockSpec((seq, ch), lambda b: (b, 0)), _resident(dw.shape), _resident(dwb.shape),
                  _resident(lng.shape), _resident(lnb.shape)],
        out_specs=pl.BlockSpec((seq, ch), lambda b: (b, 0)),
        out_shape=jax.ShapeDtypeStruct((n, ch), BF16),
        scratch_shapes=[pltpu.VMEM((seq + CONV_PAD + 8, ch), F32)],
        compiler_params=_cparams(("parallel",)), name="conv",
    )(u, dw, dwb, lng, lnb)


def _attn_kernel(qn_ref, kvn_ref, kr_ref, wqn_ref, wqr_ref, wqrs_ref, wk_ref, wv_ref, cos_ref, sin_ref,
                 o_ref, q_s, k_s, v_s, *, seq, tq):
    qn = qn_ref[...]
    kvn = kvn_ref[...]
    scale = 1.0 / math.sqrt(QK_NOPE + QK_ROPE)

    def proj(a, w_ref):
        return jnp.dot(a, w_ref[0], preferred_element_type=F32)

    q_s[:, 0:QK_NOPE] = (proj(qn, wqn_ref) * scale).astype(BF16)
    q_rope = proj(qn, wqr_ref) * cos_ref[...] + proj(qn, wqrs_ref) * sin_ref[...]
    q_s[:, QK_NOPE:QK_NOPE + QK_ROPE] = (q_rope * scale).astype(BF16)
    k_s[:, 0:QK_NOPE] = proj(kvn, wk_ref).astype(BF16)
    k_s[:, QK_NOPE:QK_NOPE + QK_ROPE] = kr_ref[...]
    v_s[...] = proj(kvn, wv_ref).astype(BF16)

    for i in range(seq // tq):
        kend = (i + 1) * tq
        q = q_s[i * tq:kend, :]
        s = lax.dot_general(q, k_s[0:kend, :], NT_DIMS, preferred_element_type=F32)
        q_chunk = (lax.broadcasted_iota(jnp.int32, (tq, kend), 0) + i * tq) // CHUNK
        k_chunk = lax.broadcasted_iota(jnp.int32, (tq, kend), 1) // CHUNK
        s = jnp.where(q_chunk >= k_chunk, s, -jnp.inf)
        m = jnp.max(s, axis=-1, keepdims=True)
        p = jnp.exp(s - m)
        l = jnp.sum(p, axis=-1, keepdims=True)
        o = jnp.dot(p.astype(BF16), v_s[0:kend, :], preferred_element_type=F32) / l
        o_ref[i * tq:kend, :] = o.astype(BF16)


def _attention(qn, kvn, kr, wqn, wqr, wqrs, wk, wv, cos2, sin2, seq, tq=256):
    n, lora = qn.shape
    heads = wqn.shape[0]
    seq_blk = lambda c: pl.BlockSpec((seq, c), lambda b, h: (b, 0))
    head_w = lambda w: pl.BlockSpec((1,) + w.shape[1:], lambda b, h: (h, 0, 0))
    dqk = QK_NOPE + QK_ROPE
    return pl.pallas_call(
        functools.partial(_attn_kernel, seq=seq, tq=tq),
        grid=(n // seq, heads),
        in_specs=[seq_blk(lora), seq_blk(lora), seq_blk(QK_ROPE), head_w(wqn), head_w(wqr), head_w(wqrs),
                  head_w(wk), head_w(wv), _resident(cos2.shape), _resident(sin2.shape)],
        out_specs=pl.BlockSpec((seq, V_DIM), lambda b, h: (b, h)),
        out_shape=jax.ShapeDtypeStruct((n, heads * V_DIM), BF16),
        scratch_shapes=[pltpu.VMEM((seq, dqk), BF16), pltpu.VMEM((seq, dqk), BF16), pltpu.VMEM((seq, V_DIM), BF16)],
        compiler_params=_cparams(("parallel", "parallel")), name="attn",
    )(qn, kvn, kr, wqn, wqr, wqrs, wk, wv, cos2, sin2)


def _merge_kernel(x_ref, ca_ref, o_ref, sgc_ref, sgm_ref, wpw_ref, wo_ref, wout_ref, gffn_ref, wpq_ref,
                  h_ref, xn_ref, q_ref):
    y_conv = jnp.dot(ca_ref[...], wpw_ref[...], preferred_element_type=F32)
    y_mla = jnp.dot(o_ref[...], wo_ref[...], preferred_element_type=F32)
    merged = sgc_ref[...].astype(F32) * y_conv + sgm_ref[...].astype(F32) * y_mla
    h = x_ref[...] + jnp.dot(merged.astype(BF16), wout_ref[...], preferred_element_type=F32)
    h_ref[...] = h
    xn = _rms(h, gffn_ref[...])
    xn_ref[...] = xn
    q_ref[...] = jnp.dot(xn.astype(BF16), wpq_ref[...], preferred_element_type=F32).astype(BF16)


def _merge(x, ca, o, sgc, sgm, wpw, wo, wout, gffn, wpq, tm=256):
    n, d = x.shape
    qd = wpq.shape[1]
    row = lambda c: pl.BlockSpec((tm, c), lambda i: (i, 0))
    return pl.pallas_call(
        _merge_kernel, grid=(n // tm,),
        in_specs=[row(d), row(d), row(d), row(d), row(d), _resident(wpw.shape), _resident(wo.shape),
                  _resident(wout.shape), _resident(gffn.shape), _resident(wpq.shape)],
        out_specs=(row(d), row(d), row(qd)),
        out_shape=(jax.ShapeDtypeStruct((n, d), F32), jax.ShapeDtypeStruct((n, d), F32),
                   jax.ShapeDtypeStruct((n, qd), BF16)),
        compiler_params=_cparams(("parallel",)), name="merge",
    )(x, ca, o, sgc, sgm, wpw, wo, wout, gffn, wpq)


_CAND = [(a, b) for a in range(PEER_TOPK) for b in range(PEER_TOPK) if (a + 1) * (b + 1) <= PEER_TOPK]
N_CAND = len(_CAND)
N_CAND_PAD = -(-N_CAND // 8) * 8
_POS_PAD = PEER_TOPK * PEER_TOPK


def _route_kernel(q_ref, keys_ref, pos_ref, idx_ref, g_ref, cand_s, cand_i, idx_s, g_s):
    tt = q_ref.shape[0]
    key_iota = lax.broadcasted_iota(jnp.int32, (PEER_KEYS, tt), 0)
    neg = -jnp.inf
    pos = pos_ref[...]
    cand_s[N_CAND_PAD - 8:N_CAND_PAD, :] = jnp.full((8, tt), neg, F32)
    cand_i[N_CAND_PAD - 8:N_CAND_PAD, :] = jnp.zeros((8, tt), jnp.int32)

    for h in range(PEER_HEADS):
        halves = []
        for c in range(2):
            hc = 2 * h + c
            sc = lax.dot_general(keys_ref[hc], q_ref[:, hc * PEER_KEYS:(hc + 1) * PEER_KEYS], NT_DIMS,
                                 preferred_element_type=F32)
            vals, ids = [], []
            for _ in range(PEER_TOPK):
                m = jnp.max(sc, axis=0, keepdims=True)
                am = jnp.min(jnp.where(sc == m, key_iota, PEER_KEYS), axis=0, keepdims=True)
                sc = jnp.where(key_iota == am, neg, sc)
                vals.append(m)
                ids.append(am)
            halves.append((vals, ids))
        (s1, i1), (s2, i2) = halves
        for n, (a, b) in enumerate(_CAND):
            cand_s[n:n + 1, :] = s1[a] + s2[b]
            cand_i[n:n + 1, :] = i1[a] * PEER_KEYS + i2[b]
        cs = cand_s[...]
        ci = cand_i[...]
        top_s, top_e = [], []
        for _ in range(PEER_TOPK):
            m = jnp.max(cs, axis=0, keepdims=True)
            am = jnp.min(jnp.where(cs == m, pos, _POS_PAD), axis=0, keepdims=True)
            sel = pos == am
            top_e.append(jnp.sum(jnp.where(sel, ci, 0), axis=0, keepdims=True))
            cs = jnp.where(sel, neg, cs)
            top_s.append(m)
        ex = [jnp.exp(s - top_s[0]) for s in top_s]
        denom = ex[0]
        for e in ex[1:]:
            denom = denom + e
        for r in range(PEER_TOPK):
            j = h * PEER_TOPK + r
            idx_s[j:j + 1, :] = top_e[r] * ROW_WORDS
            gate = ex[r] / denom
            g_s[2 * j:2 * j + 1, :] = gate
            g_s[2 * j + 1:2 * j + 2, :] = gate
    idx_ref[...] = idx_s[...].T
    g_ref[...] = g_s[...].T


def _route(q, keys, tt=128):
    n, qd = q.shape
    pos = np.full((N_CAND_PAD, tt), _POS_PAD, np.int32)
    for r, (a, b) in enumerate(_CAND):
        pos[r, :] = a * PEER_TOPK + b
    return pl.pallas_call(
        _route_kernel, grid=(n // tt,),
        in_specs=[pl.BlockSpec((tt, qd), lambda i: (i, 0)), _resident(keys.shape), _resident(pos.shape)],
        out_specs=(pl.BlockSpec((tt, N_PAIR), lambda i: (i, 0)), pl.BlockSpec((tt, 2 * N_PAIR), lambda i: (i, 0))),
        out_shape=(jax.ShapeDtypeStruct((n, N_PAIR), jnp.int32), jax.ShapeDtypeStruct((n, 2 * N_PAIR), F32)),
        scratch_shapes=[pltpu.VMEM((N_CAND_PAD, tt), F32), pltpu.VMEM((N_CAND_PAD, tt), jnp.int32),
                        pltpu.VMEM((N_PAIR, tt), jnp.int32), pltpu.VMEM((2 * N_PAIR, tt), F32)],
        compiler_params=_cparams(("parallel",)), name="route",
    )(q, keys, jnp.asarray(pos))


def _pack_table(tab):
    e, d = tab.shape
    bits = lax.bitcast_convert_type(tab.astype(BF16), jnp.uint16).astype(jnp.uint32)
    words = bits[:, :d // 2] | (bits[:, d // 2:] << 16)
    return words.reshape(e * ROW_WORDS, LANES)


def _gather_tile(idx_ref, tab_ref, tile_ref, t):
    for j in range(N_PAIR):
        e = idx_ref[t, j]
        tile_ref[pl.ds(j, ROW_WORDS, stride=TILE_STRIDE), :] = tab_ref[pl.ds(pl.multiple_of(e, ROW_WORDS), ROW_WORDS), :]
    words = jnp.concatenate([tile_ref[s * TILE_STRIDE:s * TILE_STRIDE + N_PAIR, :] for s in range(ROW_WORDS)], axis=1)
    return pltpu.bitcast(words, BF16)


def _split_bf16(x):
    hi = x.astype(BF16).astype(F32)
    lo = (x - hi).astype(BF16).astype(F32)
    return hi, lo


def _peer_u_kernel(idx_ref, xn_ref, tab_ref, out_ref, tile0, tile1, *, tb):
    x = xn_ref[...]
    half = x.shape[1] // 2
    a_hi, a_lo = _split_bf16(x[:, :half])
    b_hi, b_lo = _split_bf16(x[:, half:])
    lhs = jnp.concatenate([a_hi, a_lo, b_hi, b_lo], axis=0).astype(BF16)
    row = lax.broadcasted_iota(jnp.int32, (tb, 2 * N_PAIR), 0)
    lane = lax.broadcasted_iota(jnp.int32, (tb, 2 * N_PAIR), 1)
    acc = jnp.zeros((tb, 2 * N_PAIR), F32)
    for t in range(tb):
        rows = _gather_tile(idx_ref, tab_ref, (tile0, tile1)[t % 2], t)
        r = lax.dot_general(lhs, rows, NT_DIMS, preferred_element_type=F32)
        first = r[0:tb] + r[tb:2 * tb]
        second = r[2 * tb:3 * tb] + r[3 * tb:4 * tb]
        val = first + pltpu.roll(second, 2 * N_PAIR - 1, axis=1)
        acc = acc + jnp.where(row == t, val, 0.0)
    out_ref[...] = jnp.where(lane % 2 == 0, acc, pltpu.roll(acc, 1, axis=1))


def _peer_u(idx, xn, tab, tb=8):
    n, d = xn.shape
    return pl.pallas_call(
        functools.partial(_peer_u_kernel, tb=tb), grid=(n // tb,),
        in_specs=[pl.BlockSpec((tb, N_PAIR), lambda i: (i, 0), memory_space=pltpu.SMEM),
                  pl.BlockSpec((tb, d), lambda i: (i, 0)), _resident(tab.shape)],
        out_specs=pl.BlockSpec((tb, 2 * N_PAIR), lambda i: (i, 0)),
        out_shape=jax.ShapeDtypeStruct((n, 2 * N_PAIR), F32),
        scratch_shapes=[pltpu.VMEM((ROW_WORDS * TILE_STRIDE, LANES), jnp.uint32)] * 2,
        compiler_params=_cparams(("parallel",)), name="peer_u",
    )(idx, xn, tab)


def _peer_v_kernel(idx_ref, act_ref, g_ref, h_ref, gfin_ref, tab_ref, out_ref, tile0, tile1, *, tb):
    a = jax.nn.gelu(act_ref[...], approximate=True) * g_ref[...]
    a_hi, a_lo = _split_bf16(a)
    lane = lax.broadcasted_iota(jnp.int32, a.shape, 1)
    even = lane % 2 == 0
    zero = jnp.zeros_like(a)
    lhs = jnp.concatenate([jnp.where(even, a_hi, zero), jnp.where(even, a_lo, zero),
                           jnp.where(even, zero, a_hi), jnp.where(even, zero, a_lo)], axis=0).astype(BF16)
    half = h_ref.shape[1] // 2
    row = lax.broadcasted_iota(jnp.int32, (tb, half), 0)
    acc_a = jnp.zeros((tb, half), F32)
    acc_b = jnp.zeros((tb, half), F32)
    for t in range(tb):
        rows = _gather_tile(idx_ref, tab_ref, (tile0, tile1)[t % 2], t)
        r = jnp.dot(lhs, rows, preferred_element_type=F32)
        mine = row == t
        acc_a = acc_a + jnp.where(mine, r[0:tb] + r[tb:2 * tb], 0.0)
        acc_b = acc_b + jnp.where(mine, r[2 * tb:3 * tb] + r[3 * tb:4 * tb], 0.0)
    y = h_ref[...] + jnp.concatenate([acc_a, acc_b], axis=1)
    out_ref[...] = _rms(y, gfin_ref[...])


def _peer_v(idx, act, g, h, gfin, tab, tb=8):
    n, d = h.shape
    row = lambda c: pl.BlockSpec((tb, c), lambda i: (i, 0))
    return pl.pallas_call(
        functools.partial(_peer_v_kernel, tb=tb), grid=(n // tb,),
        in_specs=[pl.BlockSpec((tb, N_PAIR), lambda i: (i, 0), memory_space=pltpu.SMEM),
                  row(2 * N_PAIR), row(2 * N_PAIR), row(d), _resident(gfin.shape), _resident(tab.shape)],
        out_specs=row(d),
        out_shape=jax.ShapeDtypeStruct((n, d), F32),
        scratch_shapes=[pltpu.VMEM((ROW_WORDS * TILE_STRIDE, LANES), jnp.uint32)] * 2,
        compiler_params=_cparams(("parallel",)), name="peer_v",
    )(idx, act, g, h, gfin, tab)


def _rope_tables(seq):
    pos = jnp.arange(seq, dtype=F32)
    inv_freq = ROPE_THETA ** (-jnp.arange(0, QK_ROPE, 2, dtype=F32) / QK_ROPE)
    ang = pos[:, None] * inv_freq[None, :]
    cos, sin = jnp.cos(ang), jnp.sin(ang)
    return jnp.concatenate([cos, cos], axis=-1), jnp.concatenate([-sin, sin], axis=-1)


def _swap_halves(w):
    half = w.shape[-1] // 2
    return jnp.concatenate([w[..., half:], w[..., :half]], axis=-1)


def _layer(x, seq, cos2, sin2, norm_mix_g, w_in, b_gate, conv_dw, conv_dw_b, conv_ln_g, conv_ln_b, conv_w_pw,
           q_norm_g, w_q_up, kv_norm_g, w_kv_up, mla_w_o, w_out, norm_ffn_g, peer_w_q, peer_sub_keys, peer_u,
           peer_v, out_gain):
    d = x.shape[1]
    lora = q_norm_g.shape[0]
    row = lambda v: v.reshape(1, -1)
    sizes = (d, d, lora, lora, QK_ROPE, d, d)
    splits = [int(i) for i in np.cumsum(sizes)[:-1]]
    wa, wb, wq, wkv, wkr, wgc, wgm = [w.astype(BF16) for w in jnp.split(w_in, splits, axis=1)]
    u, qn, kvn, kr, sgc, sgm = _in_proj(
        x, row(norm_mix_g), (wa, wb, wq, wkv, wkr, _swap_halves(wkr), wgc, wgm), b_gate, row(q_norm_g),
        row(kv_norm_g), cos2, sin2, seq)

    ca = _conv(u, conv_dw, row(conv_dw_b), row(conv_ln_g), row(conv_ln_b), seq)

    wq_h = w_q_up.reshape(lora, MLA_HEADS, QK_NOPE + QK_ROPE).transpose(1, 0, 2).astype(BF16)
    wkv_h = w_kv_up.reshape(lora, MLA_HEADS, QK_NOPE + V_DIM).transpose(1, 0, 2).astype(BF16)
    wqr = wq_h[:, :, QK_NOPE:]
    o = _attention(qn, kvn, kr, wq_h[:, :, :QK_NOPE], wqr, _swap_halves(wqr), wkv_h[:, :, :QK_NOPE],
                   wkv_h[:, :, QK_NOPE:], cos2, sin2, seq)

    h, xn, q = _merge(x, ca, o, sgc, sgm, conv_w_pw.astype(BF16), mla_w_o.astype(BF16), w_out.astype(BF16),
                      row(norm_ffn_g), peer_w_q.astype(BF16))

    keys = peer_sub_keys.reshape(2 * PEER_HEADS, PEER_KEYS, -1).astype(BF16)
    idx, g = _route(q, keys)
    act = _peer_u(idx, xn, _pack_table(peer_u))
    return _peer_v(idx, act, g, h, row(out_gain), _pack_table(peer_v))


def kernel(x, norm_mix_g, w_in, b_gate, conv_dw, conv_dw_b, conv_ln_g, conv_ln_b, conv_w_pw, q_norm_g, w_q_up, kv_norm_g, w_kv_up, mla_w_o, w_out, norm_ffn_g, peer_w_q, peer_sub_keys, peer_u, peer_v, norm_final_g):
    b, s, d = x.shape
    depth = w_in.shape[0]
    assert depth == 1, "the last layer's PEER kernel applies the final norm; deeper stacks need a plain-gain variant"
    cos2, sin2 = _rope_tables(s)
    h = _layer(x.reshape(b * s, d), s, cos2, sin2, norm_mix_g[0], w_in[0], b_gate[0], conv_dw[0], conv_dw_b[0],
               conv_ln_g[0], conv_ln_b[0], conv_w_pw[0], q_norm_g[0], w_q_up[0], kv_norm_g[0], w_kv_up[0],
               mla_w_o[0], w_out[0], norm_ffn_g[0], peer_w_q[0], peer_sub_keys[0], peer_u[0], peer_v[0],
               norm_final_g)
    return h.reshape(b, s, d)
```

```python
import functools
import math

import jax
import jax.numpy as jnp
import numpy as np
from jax import lax
from jax.experimental import pallas as pl
from jax.experimental.pallas import tpu as pltpu

EPS = 1e-6
CHUNK = 64
CONV_WIDTH = 31
MLA_HEADS = 8
QK_NOPE = 128
QK_ROPE = 64
V_DIM = 128
ROPE_THETA = 10000.0
PEER_HEADS = 8
PEER_KEYS = 128
PEER_TOPK = 16
N_PAIR = PEER_HEADS * PEER_TOPK

LANES = 128
ROW_WORDS = 4
TILE_STRIDE = 136
VMEM_LIMIT = 56 * 1024 * 1024

F32 = jnp.float32
BF16 = jnp.bfloat16
NT_DIMS = (((1,), (1,)), ((), ()))


def _cparams(sem):
    return pltpu.CompilerParams(dimension_semantics=sem, vmem_limit_bytes=VMEM_LIMIT)


def _rms(x, g):
    return x * lax.rsqrt(jnp.mean(x * x, axis=-1, keepdims=True) + EPS) * g


def _resident(shape):
    nd = len(shape)
    return pl.BlockSpec(shape, lambda *_: (0,) * nd, pipeline_mode=pl.Buffered(1))


def _inproj_kernel(x_ref, gmix_ref, wa_ref, wb_ref, wq_ref, wkv_ref, wkr_ref, wkrs_ref, wgc_ref, wgm_ref,
                   bg_ref, qg_ref, kvg_ref, cos_ref, sin_ref,
                   u_ref, qn_ref, kvn_ref, kr_ref, sgc_ref, sgm_ref):
    xn = _rms(x_ref[...], gmix_ref[...]).astype(BF16)

    def proj(w_ref):
        return jnp.dot(xn, w_ref[...], preferred_element_type=F32)

    u_ref[...] = proj(wa_ref) * jax.nn.sigmoid(proj(wb_ref))
    qn_ref[...] = _rms(proj(wq_ref), qg_ref[...]).astype(BF16)
    kvn_ref[...] = _rms(proj(wkv_ref), kvg_ref[...]).astype(BF16)
    kr_ref[...] = (proj(wkr_ref) * cos_ref[...] + proj(wkrs_ref) * sin_ref[...]).astype(BF16)
    sgc_ref[...] = jax.nn.sigmoid(proj(wgc_ref) + bg_ref[0:1, :]).astype(BF16)
    sgm_ref[...] = jax.nn.sigmoid(proj(wgm_ref) + bg_ref[1:2, :]).astype(BF16)


def _in_proj(x, gmix, ws, bg, qg, kvg, cos2, sin2, seq, tm=256):
    n, d = x.shape
    pos_blocks = seq // tm
    row = lambda c: pl.BlockSpec((tm, c), lambda i: (i, 0))
    pos = pl.BlockSpec((tm, QK_ROPE), lambda i: (i % pos_blocks, 0))
    in_specs = [row(d), _resident(gmix.shape)] + [_resident(w.shape) for w in ws]
    in_specs += [_resident(bg.shape), _resident(qg.shape), _resident(kvg.shape), pos, pos]
    q_lora = qg.shape[1]
    out_shape = (
        jax.ShapeDtypeStruct((n, d), F32),
        jax.ShapeDtypeStruct((n, q_lora), BF16),
        jax.ShapeDtypeStruct((n, q_lora), BF16),
        jax.ShapeDtypeStruct((n, QK_ROPE), BF16),
        jax.ShapeDtypeStruct((n, d), BF16),
        jax.ShapeDtypeStruct((n, d), BF16),
    )
    out_specs = (row(d), row(q_lora), row(q_lora), row(QK_ROPE), row(d), row(d))
    return pl.pallas_call(
        _inproj_kernel, grid=(n // tm,), in_specs=in_specs, out_specs=out_specs, out_shape=out_shape,
        compiler_params=_cparams(("parallel",)), name="in_proj",
    )(x, gmix, *ws, bg, qg, kvg, cos2, sin2)


CONV_PAD = 32
CONV_HALO = 40


def _conv_kernel(u_ref, dw_ref, dwb_ref, lng_ref, lnb_ref, out_ref, pad_ref, *, seq, tt):
    ch = u_ref.shape[1]
    pad_ref[0:CONV_PAD, :] = jnp.zeros((CONV_PAD, ch), F32)
    pad_ref[CONV_PAD:CONV_PAD + seq, :] = u_ref[...]
    pad_ref[CONV_PAD + seq:CONV_PAD + seq + 8, :] = jnp.zeros((8, ch), F32)
    first = CONV_PAD - (CONV_WIDTH - 1)

    def tile(i, carry):
        t0 = pl.multiple_of(i * tt, tt)
        win = pad_ref[pl.ds(t0, tt + CONV_HALO), :]
        acc = jnp.broadcast_to(dwb_ref[...], (tt, ch))
        for s in range(8):
            taps = [w for w in range(CONV_WIDTH) if (first + w) % 8 == s]
            if not taps:
                continue
            shifted = win[s:s + tt + CONV_PAD, :]
            for w in taps:
                k = (first + w) // 8 * 8
                acc = acc + dw_ref[w:w + 1, :] * shifted[k:k + tt, :]
        mu = jnp.mean(acc, axis=-1, keepdims=True)
        cen = acc - mu
        var = jnp.mean(cen * cen, axis=-1, keepdims=True)
        y = cen * lax.rsqrt(var + EPS) * lng_ref[...] + lnb_ref[...]
        out_ref[pl.ds(t0, tt), :] = (y * jax.nn.sigmoid(y)).astype(BF16)
        return carry

    lax.fori_loop(0, seq // tt, tile, 0)


def _conv(u, dw, dwb, lng, lnb, seq, tt=32):
    n, ch = u.shape
    return pl.pallas_call(
        functools.partial(_conv_kernel, seq=seq, tt=tt),
        grid=(n // seq,),
        in_specs=[pl.BlockSpec((seq, ch), lambda b: (b, 0)), _resident(dw.shape), _resident(dwb.shape),
                  _resident(lng.shape), _resident(lnb.shape)],
        out_specs=pl.BlockSpec((seq, ch), lambda b: (b, 0)),
        out_shape=jax.ShapeDtypeStruct((n, ch), BF16),
        scratch_shapes=[pltpu.VMEM((seq + CONV_PAD + 8, ch), F32)],
        compiler_params=_cparams(("parallel",)), name="conv",
    )(u, dw, dwb, lng, lnb)


def _attn_kernel(qn_ref, kvn_ref, kr_ref, wqn_ref, wqr_ref, wqrs_ref, wk_ref, wv_ref, cos_ref, sin_ref,
                 o_ref, q_s, k_s, v_s, *, seq, tq):
    qn = qn_ref[...]
    kvn = kvn_ref[...]
    scale = 1.0 / math.sqrt(QK_NOPE + QK_ROPE)

    def proj(a, w_ref):
        return jnp.dot(a, w_ref[0], preferred_element_type=F32)

    q_s[:, 0:QK_NOPE] = (proj(qn, wqn_ref) * scale).astype(BF16)
    q_rope = proj(qn, wqr_ref) * cos_ref[...] + proj(qn, wqrs_ref) * sin_ref[...]
    q_s[:, QK_NOPE:QK_NOPE + QK_ROPE] = (q_rope * scale).astype(BF16)
    k_s[:, 0:QK_NOPE] = proj(kvn, wk_ref).astype(BF16)
    k_s[:, QK_NOPE:QK_NOPE + QK_ROPE] = kr_ref[...]
    v_s[...] = proj(kvn, wv_ref).astype(BF16)

    for i in range(seq // tq):
        kend = (i + 1) * tq
        q = q_s[i * tq:kend, :]
        s = lax.dot_general(q, k_s[0:kend, :], NT_DIMS, preferred_element_type=F32)
        q_chunk = (lax.broadcasted_iota(jnp.int32, (tq, kend), 0) + i * tq) // CHUNK
        k_chunk = lax.broadcasted_iota(jnp.int32, (tq, kend), 1) // CHUNK
        s = jnp.where(q_chunk >= k_chunk, s, -jnp.inf)
        m = jnp.max(s, axis=-1, keepdims=True)
        p = jnp.exp(s - m)
        l = jnp.sum(p, axis=-1, keepdims=True)
        o = jnp.dot(p.astype(BF16), v_s[0:kend, :], preferred_element_type=F32) / l
        o_ref[i * tq:kend, :] = o.astype(BF16)


def _attention(qn, kvn, kr, wqn, wqr, wqrs, wk, wv, cos2, sin2, seq, tq=256):
    n, lora = qn.shape
    heads = wqn.shape[0]
    seq_blk = lambda c: pl.BlockSpec((seq, c), lambda b, h: (b, 0))
    head_w = lambda w: pl.BlockSpec((1,) + w.shape[1:], lambda b, h: (h, 0, 0))
    dqk = QK_NOPE + QK_ROPE
    return pl.pallas_call(
        functools.partial(_attn_kernel, seq=seq, tq=tq),
        grid=(n // seq, heads),
        in_specs=[seq_blk(lora), seq_blk(lora), seq_blk(QK_ROPE), head_w(wqn), head_w(wqr), head_w(wqrs),
                  head_w(wk), head_w(wv), _resident(cos2.shape), _resident(sin2.shape)],
        out_specs=pl.BlockSpec((seq, V_DIM), lambda b, h: (b, h)),
        out_shape=jax.ShapeDtypeStruct((n, heads * V_DIM), BF16),
        scratch_shapes=[pltpu.VMEM((seq, dqk), BF16), pltpu.VMEM((seq, dqk), BF16), pltpu.VMEM((seq, V_DIM), BF16)],
        compiler_params=_cparams(("parallel", "parallel")), name="attn",
    )(qn, kvn, kr, wqn, wqr, wqrs, wk, wv, cos2, sin2)


def _merge_kernel(x_ref, ca_ref, o_ref, sgc_ref, sgm_ref, wpw_ref, wo_ref, wout_ref, gffn_ref, wpq_ref,
                  h_ref, xn_ref, q_ref):
    y_conv = jnp.dot(ca_ref[...], wpw_ref[...], preferred_element_type=F32)
    y_mla = jnp.dot(o_ref[...], wo_ref[...], preferred_element_type=F32)
    merged = sgc_ref[...].astype(F32) * y_conv + sgm_ref[...].astype(F32) * y_mla
    h = x_ref[...] + jnp.dot(merged.astype(BF16), wout_ref[...], preferred_element_type=F32)
    h_ref[...] = h
    xn = _rms(h, gffn_ref[...])
    xn_ref[...] = xn
    q_ref[...] = jnp.dot(xn.astype(BF16), wpq_ref[...], preferred_element_type=F32).astype(BF16)


def _merge(x, ca, o, sgc, sgm, wpw, wo, wout, gffn, wpq, tm=256):
    n, d = x.shape
    qd = wpq.shape[1]
    row = lambda c: pl.BlockSpec((tm, c), lambda i: (i, 0))
    return pl.pallas_call(
        _merge_kernel, grid=(n // tm,),
        in_specs=[row(d), row(d), row(d), row(d), row(d), _resident(wpw.shape), _resident(wo.shape),
                  _resident(wout.shape), _resident(gffn.shape), _resident(wpq.shape)],
        out_specs=(row(d), row(d), row(qd)),
        out_shape=(jax.ShapeDtypeStruct((n, d), F32), jax.ShapeDtypeStruct((n, d), F32),
                   jax.ShapeDtypeStruct((n, qd), BF16)),
        compiler_params=_cparams(("parallel",)), name="merge",
    )(x, ca, o, sgc, sgm, wpw, wo, wout, gffn, wpq)


_CAND = [(a, b) for a in range(PEER_TOPK) for b in range(PEER_TOPK) if (a + 1) * (b + 1) <= PEER_TOPK]
N_CAND = len(_CAND)
N_CAND_PAD = -(-N_CAND // 8) * 8
_POS_PAD = PEER_TOPK * PEER_TOPK


def _route_kernel(q_ref, keys_ref, pos_ref, idx_ref, g_ref, cand_s, cand_i, idx_s, g_s):
    tt = q_ref.shape[0]
    key_iota = lax.broadcasted_iota(jnp.int32, (PEER_KEYS, tt), 0)
    neg = -jnp.inf
    pos = pos_ref[...]
    cand_s[N_CAND_PAD - 8:N_CAND_PAD, :] = jnp.full((8, tt), neg, F32)
    cand_i[N_CAND_PAD - 8:N_CAND_PAD, :] = jnp.zeros((8, tt), jnp.int32)

    for h in range(PEER_HEADS):
        halves = []
        for c in range(2):
            hc = 2 * h + c
            sc = lax.dot_general(keys_ref[hc], q_ref[:, hc * PEER_KEYS:(hc + 1) * PEER_KEYS], NT_DIMS,
                                 preferred_element_type=F32)
            vals, ids = [], []
            for _ in range(PEER_TOPK):
                m = jnp.max(sc, axis=0, keepdims=True)
                am = jnp.min(jnp.where(sc == m, key_iota, PEER_KEYS), axis=0, keepdims=True)
                sc = jnp.where(key_iota == am, neg, sc)
                vals.append(m)
                ids.append(am)
            halves.append((vals, ids))
        (s1, i1), (s2, i2) = halves
        for n, (a, b) in enumerate(_CAND):
            cand_s[n:n + 1, :] = s1[a] + s2[b]
            cand_i[n:n + 1, :] = i1[a] * PEER_KEYS + i2[b]
        cs = cand_s[...]
        ci = cand_i[...]
        top_s, top_e = [], []
        for _ in range(PEER_TOPK):
            m = jnp.max(cs, axis=0, keepdims=True)
            am = jnp.min(jnp.where(cs == m, pos, _POS_PAD), axis=0, keepdims=True)
            sel = pos == am
            top_e.append(jnp.sum(jnp.where(sel, ci, 0), axis=0, keepdims=True))
            cs = jnp.where(sel, neg, cs)
            top_s.append(m)
        ex = [jnp.exp(s - top_s[0]) for s in top_s]
        denom = ex[0]
        for e in ex[1:]:
            denom = denom + e
        for r in range(PEER_TOPK):
            j = h * PEER_TOPK + r
            idx_s[j:j + 1, :] = top_e[r] * ROW_WORDS
            gate = ex[r] / denom
            g_s[2 * j:2 * j + 1, :] = gate
            g_s[2 * j + 1:2 * j + 2, :] = gate
    idx_ref[...] = idx_s[...].T
    g_ref[...] = g_s[...].T


def _route(q, keys, tt=128):
    n, qd = q.shape
    pos = np.full((N_CAND_PAD, tt), _POS_PAD, np.int32)
    for r, (a, b) in enumerate(_CAND):
        pos[r, :] = a * PEER_TOPK + b
    return pl.pallas_call(
        _route_kernel, grid=(n // tt,),
        in_specs=[pl.BlockSpec((tt, qd), lambda i: (i, 0)), _resident(keys.shape), _resident(pos.shape)],
        out_specs=(pl.BlockSpec((tt, N_PAIR), lambda i: (i, 0)), pl.BlockSpec((tt, 2 * N_PAIR), lambda i: (i, 0))),
        out_shape=(jax.ShapeDtypeStruct((n, N_PAIR), jnp.int32), jax.ShapeDtypeStruct((n, 2 * N_PAIR), F32)),
        scratch_shapes=[pltpu.VMEM((N_CAND_PAD, tt), F32), pltpu.VMEM((N_CAND_PAD, tt), jnp.int32),
                        pltpu.VMEM((N_PAIR, tt), jnp.int32), pltpu.VMEM((2 * N_PAIR, tt), F32)],
        compiler_params=_cparams(("parallel",)), name="route",
    )(q, keys, jnp.asarray(pos))


N_SLOT = 2


def _pack_table(tab):
    e, d = tab.shape
    bits = lax.bitcast_convert_type(tab.astype(BF16), jnp.uint16).astype(jnp.uint32)
    words = bits[:, :d // 2] | (bits[:, d // 2:] << 16)
    return words.reshape(e * ROW_WORDS, LANES)


def _idx_copy(idx_hbm, idx_smem, sem, step, slot):
    return pltpu.make_async_copy(idx_hbm.at[step, slot], idx_smem.at[slot], sem.at[slot])


def _gather_rows(idx_smem, slot, t, tab_ref, tile_ref):
    for j in range(N_PAIR):
        e = idx_smem[slot, t, j]
        tile_ref[pl.ds(j, ROW_WORDS, stride=TILE_STRIDE), :] = tab_ref[pl.ds(pl.multiple_of(e, ROW_WORDS), ROW_WORDS), :]


def _read_tile(tile_ref):
    words = jnp.concatenate([tile_ref[s * TILE_STRIDE:s * TILE_STRIDE + N_PAIR, :] for s in range(ROW_WORDS)], axis=1)
    return pltpu.bitcast(words, BF16)


def _gather_pipeline(idx_hbm, idx_smem, sem, tab_ref, tiles, tb, sub, make_lhs, dot, finish, store, zero_acc):
    i = pl.program_id(0)
    last_step = pl.num_programs(0) - 1
    per_slot = tb // N_SLOT
    add = lambda a, b: jax.tree.map(jnp.add, a, b)

    @pl.when(i == 0)
    def _():
        for s in range(N_SLOT):
            _idx_copy(idx_hbm, idx_smem, sem, 0, s).start()

    accs = {}
    lhs = functools.lru_cache(maxsize=None)(make_lhs)

    def issue(item):
        k, t, tile = item
        return k, t, dot(lhs(k), _read_tile(tile))

    def retire(item):
        k, t, r = item
        accs[k] = add(accs.get(k, zero_acc), finish(r, t))
        if t == sub - 1:
            store(k, accs.pop(k))

    gathered = None
    in_flight = None
    _idx_copy(idx_hbm, idx_smem, sem, i, 0).wait()
    for slot in range(N_SLOT):
        for u in range(per_slot):
            tok = slot * per_slot + u
            _gather_rows(idx_smem, slot, u, tab_ref, tiles[tok % 2])
            if u == per_slot - 1:
                _idx_copy(idx_hbm, idx_smem, sem, jnp.minimum(i + 1, last_step), slot).start()
                if slot + 1 < N_SLOT:
                    _idx_copy(idx_hbm, idx_smem, sem, i, slot + 1).wait()
            issued = issue(gathered) if gathered is not None else None
            if in_flight is not None:
                retire(in_flight)
            in_flight = issued
            gathered = divmod(tok, sub) + (tiles[tok % 2],)
    retire(in_flight)
    retire(issue(gathered))

    @pl.when(i == last_step)
    def _():
        for s in range(N_SLOT):
            _idx_copy(idx_hbm, idx_smem, sem, last_step, s).wait()


def _split_bf16(x):
    hi = x.astype(BF16).astype(F32)
    lo = (x - hi).astype(BF16).astype(F32)
    return hi, lo


def _peer_scratch(tb):
    tile = pltpu.VMEM((ROW_WORDS * TILE_STRIDE, LANES), jnp.uint32)
    return [tile, tile, pltpu.SMEM((N_SLOT, tb // N_SLOT, N_PAIR), jnp.int32), pltpu.SemaphoreType.DMA((N_SLOT,))]


def _peer_u_kernel(idx_hbm, xn_ref, tab_ref, out_ref, tile0, tile1, idx_smem, sem, *, tb, sub):
    half = xn_ref.shape[1] // 2
    row = lax.broadcasted_iota(jnp.int32, (sub, 2 * N_PAIR), 0)
    lane = lax.broadcasted_iota(jnp.int32, (sub, 2 * N_PAIR), 1)

    def make_lhs(k):
        x = xn_ref[k * sub:(k + 1) * sub, :]
        a_hi, a_lo = _split_bf16(x[:, :half])
        b_hi, b_lo = _split_bf16(x[:, half:])
        return jnp.concatenate([a_hi, a_lo, b_hi, b_lo], axis=0).astype(BF16)

    def dot(lhs, tile):
        return lax.dot_general(lhs, tile, NT_DIMS, preferred_element_type=F32)

    def finish(r, t):
        first = r[0:sub] + r[sub:2 * sub]
        second = r[2 * sub:3 * sub] + r[3 * sub:4 * sub]
        return jnp.where(row == t, first + pltpu.roll(second, 2 * N_PAIR - 1, axis=1), 0.0)

    def store(k, acc):
        out_ref[k * sub:(k + 1) * sub, :] = jnp.where(lane % 2 == 0, acc, pltpu.roll(acc, 1, axis=1))

    _gather_pipeline(idx_hbm, idx_smem, sem, tab_ref, (tile0, tile1), tb, sub, make_lhs, dot, finish, store,
                     jnp.zeros((sub, 2 * N_PAIR), F32))


def _peer_u(idx, xn, tab, tb=64, sub=8):
    n, d = xn.shape
    return pl.pallas_call(
        functools.partial(_peer_u_kernel, tb=tb, sub=sub), grid=(n // tb,),
        in_specs=[pl.BlockSpec(memory_space=pl.ANY), pl.BlockSpec((tb, d), lambda i: (i, 0)), _resident(tab.shape)],
        out_specs=pl.BlockSpec((tb, 2 * N_PAIR), lambda i: (i, 0)),
        out_shape=jax.ShapeDtypeStruct((n, 2 * N_PAIR), F32),
        scratch_shapes=_peer_scratch(tb),
        compiler_params=_cparams(("arbitrary",)), name="peer_u",
    )(idx.reshape(n // tb, N_SLOT, tb // N_SLOT, N_PAIR), xn, tab)


def _peer_v_kernel(idx_hbm, act_ref, g_ref, h_ref, gfin_ref, tab_ref, out_ref, tile0, tile1, idx_smem, sem, *, tb, sub):
    half = h_ref.shape[1] // 2
    row = lax.broadcasted_iota(jnp.int32, (sub, half), 0)
    even = lax.broadcasted_iota(jnp.int32, (sub, 2 * N_PAIR), 1) % 2 == 0

    def make_lhs(k):
        a = jax.nn.gelu(act_ref[k * sub:(k + 1) * sub, :], approximate=True) * g_ref[k * sub:(k + 1) * sub, :]
        a_hi, a_lo = _split_bf16(a)
        zero = jnp.zeros_like(a)
        return jnp.concatenate([jnp.where(even, a_hi, zero), jnp.where(even, a_lo, zero),
                                jnp.where(even, zero, a_hi), jnp.where(even, zero, a_lo)], axis=0).astype(BF16)

    def dot(lhs, tile):
        return jnp.dot(lhs, tile, preferred_element_type=F32)

    def finish(r, t):
        mine = row == t
        return (jnp.where(mine, r[0:sub] + r[sub:2 * sub], 0.0),
                jnp.where(mine, r[2 * sub:3 * sub] + r[3 * sub:4 * sub], 0.0))

    def store(k, acc):
        y = h_ref[k * sub:(k + 1) * sub, :] + jnp.concatenate(acc, axis=1)
        out_ref[k * sub:(k + 1) * sub, :] = _rms(y, gfin_ref[...])

    _gather_pipeline(idx_hbm, idx_smem, sem, tab_ref, (tile0, tile1), tb, sub, make_lhs, dot, finish, store,
                     (jnp.zeros((sub, half), F32), jnp.zeros((sub, half), F32)))


def _peer_v(idx, act, g, h, gfin, tab, tb=64, sub=8):
    n, d = h.shape
    row = lambda c: pl.BlockSpec((tb, c), lambda i: (i, 0))
    return pl.pallas_call(
        functools.partial(_peer_v_kernel, tb=tb, sub=sub), grid=(n // tb,),
        in_specs=[pl.BlockSpec(memory_space=pl.ANY), row(2 * N_PAIR), row(2 * N_PAIR), row(d),
                  _resident(gfin.shape), _resident(tab.shape)],
        out_specs=row(d),
        out_shape=jax.ShapeDtypeStruct((n, d), F32),
        scratch_shapes=_peer_scratch(tb),
        compiler_params=_cparams(("arbitrary",)), name="peer_v",
    )(idx.reshape(n // tb, N_SLOT, tb // N_SLOT, N_PAIR), act, g, h, gfin, tab)


def _rope_tables(seq):
    pos = jnp.arange(seq, dtype=F32)
    inv_freq = ROPE_THETA ** (-jnp.arange(0, QK_ROPE, 2, dtype=F32) / QK_ROPE)
    ang = pos[:, None] * inv_freq[None, :]
    cos, sin = jnp.cos(ang), jnp.sin(ang)
    return jnp.concatenate([cos, cos], axis=-1), jnp.concatenate([-sin, sin], axis=-1)


def _swap_halves(w):
    half = w.shape[-1] // 2
    return jnp.concatenate([w[..., half:], w[..., :half]], axis=-1)


def _layer(x, seq, cos2, sin2, norm_mix_g, w_in, b_gate, conv_dw, conv_dw_b, conv_ln_g, conv_ln_b, conv_w_pw,
           q_norm_g, w_q_up, kv_norm_g, w_kv_up, mla_w_o, w_out, norm_ffn_g, peer_w_q, peer_sub_keys, peer_u,
           peer_v, out_gain):
    d = x.shape[1]
    lora = q_norm_g.shape[0]
    row = lambda v: v.reshape(1, -1)
    sizes = (d, d, lora, lora, QK_ROPE, d, d)
    splits = [int(i) for i in np.cumsum(sizes)[:-1]]
    wa, wb, wq, wkv, wkr, wgc, wgm = [w.astype(BF16) for w in jnp.split(w_in, splits, axis=1)]
    u, qn, kvn, kr, sgc, sgm = _in_proj(
        x, row(norm_mix_g), (wa, wb, wq, wkv, wkr, _swap_halves(wkr), wgc, wgm), b_gate, row(q_norm_g),
        row(kv_norm_g), cos2, sin2, seq)

    ca = _conv(u, conv_dw, row(conv_dw_b), row(conv_ln_g), row(conv_ln_b), seq)

    wq_h = w_q_up.reshape(lora, MLA_HEADS, QK_NOPE + QK_ROPE).transpose(1, 0, 2).astype(BF16)
    wkv_h = w_kv_up.reshape(lora, MLA_HEADS, QK_NOPE + V_DIM).transpose(1, 0, 2).astype(BF16)
    wqr = wq_h[:, :, QK_NOPE:]
    o = _attention(qn, kvn, kr, wq_h[:, :, :QK_NOPE], wqr, _swap_halves(wqr), wkv_h[:, :, :QK_NOPE],
                   wkv_h[:, :, QK_NOPE:], cos2, sin2, seq)

    h, xn, q = _merge(x, ca, o, sgc, sgm, conv_w_pw.astype(BF16), mla_w_o.astype(BF16), w_out.astype(BF16),
                      row(norm_ffn_g), peer_w_q.astype(BF16))

    keys = peer_sub_keys.reshape(2 * PEER_HEADS, PEER_KEYS, -1).astype(BF16)
    idx, g = _route(q, keys)
    act = _peer_u(idx, xn, _pack_table(peer_u))
    return _peer_v(idx, act, g, h, row(out_gain), _pack_table(peer_v))


def kernel(x, norm_mix_g, w_in, b_gate, conv_dw, conv_dw_b, conv_ln_g, conv_ln_b, conv_w_pw, q_norm_g, w_q_up, kv_norm_g, w_kv_up, mla_w_o, w_out, norm_ffn_g, peer_w_q, peer_sub_keys, peer_u, peer_v, norm_final_g):
    b, s, d = x.shape
    depth = w_in.shape[0]
    assert depth == 1, "the last layer's PEER kernel applies the final norm; deeper stacks need a plain-gain variant"
    cos2, sin2 = _rope_tables(s)
    h = _layer(x.reshape(b * s, d), s, cos2, sin2, norm_mix_g[0], w_in[0], b_gate[0], conv_dw[0], conv_dw_b[0],
               conv_ln_g[0], conv_ln_b[0], conv_w_pw[0], q_norm_g[0], w_q_up[0], kv_norm_g[0], w_kv_up[0],
               mla_w_o[0], w_out[0], norm_ffn_g[0], peer_w_q[0], peer_sub_keys[0], peer_u[0], peer_v[0],
               norm_final_g)
    return h.reshape(b, s, d)
```

```python
import functools
import math

import jax
import jax.numpy as jnp
import numpy as np
from jax import lax
from jax.experimental import pallas as pl
from jax.experimental.pallas import tpu as pltpu

EPS = 1e-6
CHUNK = 64
CONV_WIDTH = 31
MLA_HEADS = 8
QK_NOPE = 128
QK_ROPE = 64
V_DIM = 128
ROPE_THETA = 10000.0
PEER_HEADS = 8
PEER_KEYS = 128
PEER_TOPK = 16
N_PAIR = PEER_HEADS * PEER_TOPK

LANES = 128
ROW_WORDS = 4
TILE_STRIDE = 136
VMEM_LIMIT = 56 * 1024 * 1024

F32 = jnp.float32
BF16 = jnp.bfloat16
NT_DIMS = (((1,), (1,)), ((), ()))


def _cparams(sem):
    return pltpu.CompilerParams(dimension_semantics=sem, vmem_limit_bytes=VMEM_LIMIT)


def _rms(x, g):
    return x * lax.rsqrt(jnp.mean(x * x, axis=-1, keepdims=True) + EPS) * g


def _resident(shape):
    nd = len(shape)
    return pl.BlockSpec(shape, lambda *_: (0,) * nd, pipeline_mode=pl.Buffered(1))


def _inproj_kernel(x_ref, gmix_ref, wa_ref, wb_ref, wq_ref, wkv_ref, wkr_ref, wkrs_ref, wgc_ref, wgm_ref,
                   bg_ref, qg_ref, kvg_ref, cos_ref, sin_ref,
                   u_ref, qn_ref, kvn_ref, kr_ref, sgc_ref, sgm_ref):
    xn = _rms(x_ref[...], gmix_ref[...]).astype(BF16)

    def proj(w_ref):
        return jnp.dot(xn, w_ref[...], preferred_element_type=F32)

    u_ref[...] = proj(wa_ref) * jax.nn.sigmoid(proj(wb_ref))
    qn_ref[...] = _rms(proj(wq_ref), qg_ref[...]).astype(BF16)
    kvn_ref[...] = _rms(proj(wkv_ref), kvg_ref[...]).astype(BF16)
    kr_ref[...] = (proj(wkr_ref) * cos_ref[...] + proj(wkrs_ref) * sin_ref[...]).astype(BF16)
    sgc_ref[...] = jax.nn.sigmoid(proj(wgc_ref) + bg_ref[0:1, :]).astype(BF16)
    sgm_ref[...] = jax.nn.sigmoid(proj(wgm_ref) + bg_ref[1:2, :]).astype(BF16)


def _in_proj(x, gmix, ws, bg, qg, kvg, cos2, sin2, seq, tm=256):
    n, d = x.shape
    pos_blocks = seq // tm
    row = lambda c: pl.BlockSpec((tm, c), lambda i: (i, 0))
    pos = pl.BlockSpec((tm, QK_ROPE), lambda i: (i % pos_blocks, 0))
    in_specs = [row(d), _resident(gmix.shape)] + [_resident(w.shape) for w in ws]
    in_specs += [_resident(bg.shape), _resident(qg.shape), _resident(kvg.shape), pos, pos]
    q_lora = qg.shape[1]
    out_shape = (
        jax.ShapeDtypeStruct((n, d), F32),
        jax.ShapeDtypeStruct((n, q_lora), BF16),
        jax.ShapeDtypeStruct((n, q_lora), BF16),
        jax.ShapeDtypeStruct((n, QK_ROPE), BF16),
        jax.ShapeDtypeStruct((n, d), BF16),
        jax.ShapeDtypeStruct((n, d), BF16),
    )
    out_specs = (row(d), row(q_lora), row(q_lora), row(QK_ROPE), row(d), row(d))
    return pl.pallas_call(
        _inproj_kernel, grid=(n // tm,), in_specs=in_specs, out_specs=out_specs, out_shape=out_shape,
        compiler_params=_cparams(("parallel",)), name="in_proj",
    )(x, gmix, *ws, bg, qg, kvg, cos2, sin2)


CONV_PAD = 32
CONV_HALO = 40


def _conv_kernel(u_ref, dw_ref, dwb_ref, lng_ref, lnb_ref, out_ref, pad_ref, y_ref, *, seq, tt):
    ch = u_ref.shape[1]
    pad_ref[0:CONV_PAD, :] = jnp.zeros((CONV_PAD, ch), F32)
    pad_ref[CONV_PAD:CONV_PAD + seq, :] = u_ref[...]
    pad_ref[CONV_PAD + seq:CONV_PAD + seq + 8, :] = jnp.zeros((8, ch), F32)
    first = CONV_PAD - (CONV_WIDTH - 1)

    def tile(i, carry):
        t0 = pl.multiple_of(i * tt, tt)
        for c in range(ch // LANES):
            cols = slice(c * LANES, (c + 1) * LANES)
            win = pad_ref[pl.ds(t0, tt + CONV_HALO), cols]
            acc = jnp.broadcast_to(dwb_ref[:, cols], (tt, LANES))
            for s in range(8):
                taps = [w for w in range(CONV_WIDTH) if (first + w) % 8 == s]
                if not taps:
                    continue
                shifted = win[s:s + tt + CONV_PAD, :]
                for w in taps:
                    k = (first + w) // 8 * 8
                    acc = acc + dw_ref[w, :, cols] * shifted[k:k + tt, :]
            y_ref[:, cols] = acc
        acc = y_ref[...]
        mu = jnp.mean(acc, axis=-1, keepdims=True)
        cen = acc - mu
        var = jnp.mean(cen * cen, axis=-1, keepdims=True)
        y = cen * lax.rsqrt(var + EPS) * lng_ref[...] + lnb_ref[...]
        out_ref[pl.ds(t0, tt), :] = (y * jax.nn.sigmoid(y)).astype(BF16)
        return carry

    lax.fori_loop(0, seq // tt, tile, 0)


def _conv(u, dw, dwb, lng, lnb, seq, tt=32):
    n, ch = u.shape
    dw = jnp.broadcast_to(dw[:, None, :], (dw.shape[0], tt, ch))
    return pl.pallas_call(
        functools.partial(_conv_kernel, seq=seq, tt=tt),
        grid=(n // seq,),
        in_specs=[pl.BlockSpec((seq, ch), lambda b: (b, 0)), _resident(dw.shape), _resident(dwb.shape),
                  _resident(lng.shape), _resident(lnb.shape)],
        out_specs=pl.BlockSpec((seq, ch), lambda b: (b, 0)),
        out_shape=jax.ShapeDtypeStruct((n, ch), BF16),
        scratch_shapes=[pltpu.VMEM((seq + CONV_PAD + 8, ch), F32), pltpu.VMEM((tt, ch), F32)],
        compiler_params=_cparams(("parallel",)), name="conv",
    )(u, dw, dwb, lng, lnb)


def _attn_kernel(qn_ref, kvn_ref, kr_ref, wqn_ref, wqr_ref, wqrs_ref, wk_ref, wv_ref, cos_ref, sin_ref,
                 o_ref, q_s, k_s, v_s, *, seq, tq):
    qn = qn_ref[...]
    kvn = kvn_ref[...]
    scale = 1.0 / math.sqrt(QK_NOPE + QK_ROPE)

    def proj(a, w_ref):
        return jnp.dot(a, w_ref[0], preferred_element_type=F32)

    q_s[:, 0:QK_NOPE] = (proj(qn, wqn_ref) * scale).astype(BF16)
    q_rope = proj(qn, wqr_ref) * cos_ref[...] + proj(qn, wqrs_ref) * sin_ref[...]
    q_s[:, QK_NOPE:QK_NOPE + QK_ROPE] = (q_rope * scale).astype(BF16)
    k_s[:, 0:QK_NOPE] = proj(kvn, wk_ref).astype(BF16)
    k_s[:, QK_NOPE:QK_NOPE + QK_ROPE] = kr_ref[...]
    v_s[...] = proj(kvn, wv_ref).astype(BF16)

    visible = (lax.broadcasted_iota(jnp.int32, (tq, tq), 0) // CHUNK
               >= lax.broadcasted_iota(jnp.int32, (tq, tq), 1) // CHUNK)
    for i in range(seq // tq):
        kend = (i + 1) * tq
        q = q_s[i * tq:kend, :]
        s = lax.dot_general(q, k_s[0:kend, :], NT_DIMS, preferred_element_type=F32)
        diag = jnp.where(visible, s[:, i * tq:kend], -jnp.inf)
        s = diag if i == 0 else jnp.concatenate([s[:, :i * tq], diag], axis=1)
        m = jnp.max(s, axis=-1, keepdims=True)
        p = jnp.exp(s - m)
        l = jnp.sum(p, axis=-1, keepdims=True)
        o = jnp.dot(p.astype(BF16), v_s[0:kend, :], preferred_element_type=F32) / l
        o_ref[i * tq:kend, :] = o.astype(BF16)


def _attention(qn, kvn, kr, wqn, wqr, wqrs, wk, wv, cos2, sin2, seq, tq=256):
    n, lora = qn.shape
    heads = wqn.shape[0]
    seq_blk = lambda c: pl.BlockSpec((seq, c), lambda b, h: (b, 0))
    head_w = lambda w: pl.BlockSpec((1,) + w.shape[1:], lambda b, h: (h, 0, 0))
    dqk = QK_NOPE + QK_ROPE
    return pl.pallas_call(
        functools.partial(_attn_kernel, seq=seq, tq=tq),
        grid=(n // seq, heads),
        in_specs=[seq_blk(lora), seq_blk(lora), seq_blk(QK_ROPE), head_w(wqn), head_w(wqr), head_w(wqrs),
                  head_w(wk), head_w(wv), _resident(cos2.shape), _resident(sin2.shape)],
        out_specs=pl.BlockSpec((seq, V_DIM), lambda b, h: (b, h)),
        out_shape=jax.ShapeDtypeStruct((n, heads * V_DIM), BF16),
        scratch_shapes=[pltpu.VMEM((seq, dqk), BF16), pltpu.VMEM((seq, dqk), BF16), pltpu.VMEM((seq, V_DIM), BF16)],
        compiler_params=_cparams(("parallel", "parallel")), name="attn",
    )(qn, kvn, kr, wqn, wqr, wqrs, wk, wv, cos2, sin2)


def _merge_kernel(x_ref, ca_ref, o_ref, sgc_ref, sgm_ref, wpw_ref, wo_ref, wout_ref, gffn_ref, wpq_ref,
                  h_ref, xn_ref, q_ref):
    y_conv = jnp.dot(ca_ref[...], wpw_ref[...], preferred_element_type=F32)
    y_mla = jnp.dot(o_ref[...], wo_ref[...], preferred_element_type=F32)
    merged = sgc_ref[...].astype(F32) * y_conv + sgm_ref[...].astype(F32) * y_mla
    h = x_ref[...] + jnp.dot(merged.astype(BF16), wout_ref[...], preferred_element_type=F32)
    h_ref[...] = h
    xn = _rms(h, gffn_ref[...])
    xn_ref[...] = xn
    q_ref[...] = jnp.dot(xn.astype(BF16), wpq_ref[...], preferred_element_type=F32).astype(BF16)


def _merge(x, ca, o, sgc, sgm, wpw, wo, wout, gffn, wpq, tm=256):
    n, d = x.shape
    qd = wpq.shape[1]
    row = lambda c: pl.BlockSpec((tm, c), lambda i: (i, 0))
    return pl.pallas_call(
        _merge_kernel, grid=(n // tm,),
        in_specs=[row(d), row(d), row(d), row(d), row(d), _resident(wpw.shape), _resident(wo.shape),
                  _resident(wout.shape), _resident(gffn.shape), _resident(wpq.shape)],
        out_specs=(row(d), row(d), row(qd)),
        out_shape=(jax.ShapeDtypeStruct((n, d), F32), jax.ShapeDtypeStruct((n, d), F32),
                   jax.ShapeDtypeStruct((n, qd), BF16)),
        compiler_params=_cparams(("parallel",)), name="merge",
    )(x, ca, o, sgc, sgm, wpw, wo, wout, gffn, wpq)


_CAND = [(a, b) for a in range(PEER_TOPK) for b in range(PEER_TOPK) if (a + 1) * (b + 1) <= PEER_TOPK]
N_CAND = len(_CAND)
N_CAND_PAD = -(-N_CAND // 8) * 8
_POS_PAD = PEER_TOPK * PEER_TOPK


def _route_kernel(q_ref, keys_ref, pos_ref, idx_ref, g_ref, cand_s, cand_i, idx_s, g_s):
    tt = q_ref.shape[0]
    key_iota = lax.broadcasted_iota(jnp.int32, (PEER_KEYS, tt), 0)
    neg = -jnp.inf
    pos = pos_ref[...]
    cand_s[N_CAND_PAD - 8:N_CAND_PAD, :] = jnp.full((8, tt), neg, F32)
    cand_i[N_CAND_PAD - 8:N_CAND_PAD, :] = jnp.zeros((8, tt), jnp.int32)

    for h in range(PEER_HEADS):
        halves = []
        for c in range(2):
            hc = 2 * h + c
            sc = lax.dot_general(keys_ref[hc], q_ref[:, hc * PEER_KEYS:(hc + 1) * PEER_KEYS], NT_DIMS,
                                 preferred_element_type=F32)
            vals, ids = [], []
            for _ in range(PEER_TOPK):
                m = jnp.max(sc, axis=0, keepdims=True)
                am = jnp.min(jnp.where(sc == m, key_iota, PEER_KEYS), axis=0, keepdims=True)
                sc = jnp.where(key_iota == am, neg, sc)
                vals.append(m)
                ids.append(am)
            halves.append((vals, ids))
        (s1, i1), (s2, i2) = halves
        for n, (a, b) in enumerate(_CAND):
            cand_s[n:n + 1, :] = s1[a] + s2[b]
            cand_i[n:n + 1, :] = i1[a] * PEER_KEYS + i2[b]
        cs = cand_s[...]
        ci = cand_i[...]
        top_s, top_e = [], []
        for _ in range(PEER_TOPK):
            m = jnp.max(cs, axis=0, keepdims=True)
            am = jnp.min(jnp.where(cs == m, pos, _POS_PAD), axis=0, keepdims=True)
            sel = pos == am
            top_e.append(jnp.sum(jnp.where(sel, ci, 0), axis=0, keepdims=True))
            cs = jnp.where(sel, neg, cs)
            top_s.append(m)
        ex = [jnp.exp(s - top_s[0]) for s in top_s]
        denom = ex[0]
        for e in ex[1:]:
            denom = denom + e
        for r in range(PEER_TOPK):
            j = h * PEER_TOPK + r
            idx_s[j:j + 1, :] = top_e[r] * ROW_WORDS
            gate = ex[r] / denom
            g_s[2 * j:2 * j + 1, :] = gate
            g_s[2 * j + 1:2 * j + 2, :] = gate
    idx_ref[...] = idx_s[...].T
    g_ref[...] = g_s[...].T


def _route(q, keys, tt=128):
    n, qd = q.shape
    pos = np.full((N_CAND_PAD, tt), _POS_PAD, np.int32)
    for r, (a, b) in enumerate(_CAND):
        pos[r, :] = a * PEER_TOPK + b
    return pl.pallas_call(
        _route_kernel, grid=(n // tt,),
        in_specs=[pl.BlockSpec((tt, qd), lambda i: (i, 0)), _resident(keys.shape), _resident(pos.shape)],
        out_specs=(pl.BlockSpec((tt, N_PAIR), lambda i: (i, 0)), pl.BlockSpec((tt, 2 * N_PAIR), lambda i: (i, 0))),
        out_shape=(jax.ShapeDtypeStruct((n, N_PAIR), jnp.int32), jax.ShapeDtypeStruct((n, 2 * N_PAIR), F32)),
        scratch_shapes=[pltpu.VMEM((N_CAND_PAD, tt), F32), pltpu.VMEM((N_CAND_PAD, tt), jnp.int32),
                        pltpu.VMEM((N_PAIR, tt), jnp.int32), pltpu.VMEM((2 * N_PAIR, tt), F32)],
        compiler_params=_cparams(("parallel",)), name="route",
    )(q, keys, jnp.asarray(pos))


N_SLOT = 2


def _pack_table(tab):
    e, d = tab.shape
    bits = lax.bitcast_convert_type(tab.astype(BF16), jnp.uint16).astype(jnp.uint32)
    words = bits[:, :d // 2] | (bits[:, d // 2:] << 16)
    return words.reshape(e * ROW_WORDS, LANES)


def _idx_copy(idx_hbm, idx_smem, sem, step, slot):
    return pltpu.make_async_copy(idx_hbm.at[step, slot], idx_smem.at[slot], sem.at[slot])


def _gather_rows(idx_smem, slot, t, tab_ref, tile_ref):
    for j in range(N_PAIR):
        e = idx_smem[slot, t, j]
        tile_ref[pl.ds(j, ROW_WORDS, stride=TILE_STRIDE), :] = tab_ref[pl.ds(pl.multiple_of(e, ROW_WORDS), ROW_WORDS), :]


def _read_tile(tile_ref):
    words = jnp.concatenate([tile_ref[s * TILE_STRIDE:s * TILE_STRIDE + N_PAIR, :] for s in range(ROW_WORDS)], axis=1)
    return pltpu.bitcast(words, BF16)


def _gather_pipeline(idx_hbm, idx_smem, sem, tab_ref, tiles, tb, sub, make_lhs, dot, finish, store, zero_acc):
    i = pl.program_id(0)
    last_step = pl.num_programs(0) - 1
    per_slot = tb // N_SLOT
    add = lambda a, b: jax.tree.map(jnp.add, a, b)

    @pl.when(i == 0)
    def _():
        for s in range(N_SLOT):
            _idx_copy(idx_hbm, idx_smem, sem, 0, s).start()

    accs = {}
    lhs = functools.lru_cache(maxsize=None)(make_lhs)

    def issue(item):
        k, t, tile = item
        return k, t, dot(lhs(k), _read_tile(tile))

    def retire(item):
        k, t, r = item
        accs[k] = add(accs.get(k, zero_acc), finish(r, t))
        if t == sub - 1:
            store(k, accs.pop(k))

    gathered = None
    in_flight = None
    _idx_copy(idx_hbm, idx_smem, sem, i, 0).wait()
    for slot in range(N_SLOT):
        for u in range(per_slot):
            tok = slot * per_slot + u
            _gather_rows(idx_smem, slot, u, tab_ref, tiles[tok % 2])
            if u == per_slot - 1:
                _idx_copy(idx_hbm, idx_smem, sem, jnp.minimum(i + 1, last_step), slot).start()
                if slot + 1 < N_SLOT:
                    _idx_copy(idx_hbm, idx_smem, sem, i, slot + 1).wait()
            issued = issue(gathered) if gathered is not None else None
            if in_flight is not None:
                retire(in_flight)
            in_flight = issued
            gathered = divmod(tok, sub) + (tiles[tok % 2],)
    retire(in_flight)
    retire(issue(gathered))

    @pl.when(i == last_step)
    def _():
        for s in range(N_SLOT):
            _idx_copy(idx_hbm, idx_smem, sem, last_step, s).wait()


def _split_bf16(x):
    hi = x.astype(BF16).astype(F32)
    lo = (x - hi).astype(BF16).astype(F32)
    return hi, lo


def _peer_scratch(tb):
    tile = pltpu.VMEM((ROW_WORDS * TILE_STRIDE, LANES), jnp.uint32)
    return [tile, tile, pltpu.SMEM((N_SLOT, tb // N_SLOT, N_PAIR), jnp.int32), pltpu.SemaphoreType.DMA((N_SLOT,))]


def _peer_u_kernel(idx_hbm, xn_ref, tab_ref, out_ref, tile0, tile1, idx_smem, sem, *, tb, sub):
    half = xn_ref.shape[1] // 2
    row = lax.broadcasted_iota(jnp.int32, (sub, 2 * N_PAIR), 0)
    lane = lax.broadcasted_iota(jnp.int32, (sub, 2 * N_PAIR), 1)

    def make_lhs(k):
        x = xn_ref[k * sub:(k + 1) * sub, :]
        a_hi, a_lo = _split_bf16(x[:, :half])
        b_hi, b_lo = _split_bf16(x[:, half:])
        return jnp.concatenate([a_hi, a_lo, b_hi, b_lo], axis=0).astype(BF16)

    def dot(lhs, tile):
        return lax.dot_general(lhs, tile, NT_DIMS, preferred_element_type=F32)

    def finish(r, t):
        first = r[0:sub] + r[sub:2 * sub]
        second = r[2 * sub:3 * sub] + r[3 * sub:4 * sub]
        return jnp.where(row == t, first + pltpu.roll(second, 2 * N_PAIR - 1, axis=1), 0.0)

    def store(k, acc):
        out_ref[k * sub:(k + 1) * sub, :] = jnp.where(lane % 2 == 0, acc, pltpu.roll(acc, 1, axis=1))

    _gather_pipeline(idx_hbm, idx_smem, sem, tab_ref, (tile0, tile1), tb, sub, make_lhs, dot, finish, store,
                     jnp.zeros((sub, 2 * N_PAIR), F32))


def _peer_u(idx, xn, tab, tb=128, sub=8):
    n, d = xn.shape
    return pl.pallas_call(
        functools.partial(_peer_u_kernel, tb=tb, sub=sub), grid=(n // tb,),
        in_specs=[pl.BlockSpec(memory_space=pl.ANY), pl.BlockSpec((tb, d), lambda i: (i, 0)), _resident(tab.shape)],
        out_specs=pl.BlockSpec((tb, 2 * N_PAIR), lambda i: (i, 0)),
        out_shape=jax.ShapeDtypeStruct((n, 2 * N_PAIR), F32),
        scratch_shapes=_peer_scratch(tb),
        compiler_params=_cparams(("arbitrary",)), name="peer_u",
    )(idx.reshape(n // tb, N_SLOT, tb // N_SLOT, N_PAIR), xn, tab)


def _peer_v_kernel(idx_hbm, act_ref, g_ref, h_ref, gfin_ref, tab_ref, out_ref, tile0, tile1, idx_smem, sem, *, tb, sub):
    half = h_ref.shape[1] // 2
    row = lax.broadcasted_iota(jnp.int32, (sub, half), 0)
    even = lax.broadcasted_iota(jnp.int32, (sub, 2 * N_PAIR), 1) % 2 == 0

    def make_lhs(k):
        a = jax.nn.gelu(act_ref[k * sub:(k + 1) * sub, :], approximate=True) * g_ref[k * sub:(k + 1) * sub, :]
        a_hi, a_lo = _split_bf16(a)
        zero = jnp.zeros_like(a)
        return jnp.concatenate([jnp.where(even, a_hi, zero), jnp.where(even, a_lo, zero),
                                jnp.where(even, zero, a_hi), jnp.where(even, zero, a_lo)], axis=0).astype(BF16)

    def dot(lhs, tile):
        return jnp.dot(lhs, tile, preferred_element_type=F32)

    def finish(r, t):
        mine = row == t
        return (jnp.where(mine, r[0:sub] + r[sub:2 * sub], 0.0),
                jnp.where(mine, r[2 * sub:3 * sub] + r[3 * sub:4 * sub], 0.0))

    def store(k, acc):
        y = h_ref[k * sub:(k + 1) * sub, :] + jnp.concatenate(acc, axis=1)
        out_ref[k * sub:(k + 1) * sub, :] = _rms(y, gfin_ref[...])

    _gather_pipeline(idx_hbm, idx_smem, sem, tab_ref, (tile0, tile1), tb, sub, make_lhs, dot, finish, store,
                     (jnp.zeros((sub, half), F32), jnp.zeros((sub, half), F32)))


def _peer_v(idx, act, g, h, gfin, tab, tb=128, sub=8):
    n, d = h.shape
    row = lambda c: pl.BlockSpec((tb, c), lambda i: (i, 0))
    return pl.pallas_call(
        functools.partial(_peer_v_kernel, tb=tb, sub=sub), grid=(n // tb,),
        in_specs=[pl.BlockSpec(memory_space=pl.ANY), row(2 * N_PAIR), row(2 * N_PAIR), row(d),
                  _resident(gfin.shape), _resident(tab.shape)],
        out_specs=row(d),
        out_shape=jax.ShapeDtypeStruct((n, d), F32),
        scratch_shapes=_peer_scratch(tb),
        compiler_params=_cparams(("arbitrary",)), name="peer_v",
    )(idx.reshape(n // tb, N_SLOT, tb // N_SLOT, N_PAIR), act, g, h, gfin, tab)


def _rope_tables(seq):
    pos = jnp.arange(seq, dtype=F32)
    inv_freq = ROPE_THETA ** (-jnp.arange(0, QK_ROPE, 2, dtype=F32) / QK_ROPE)
    ang = pos[:, None] * inv_freq[None, :]
    cos, sin = jnp.cos(ang), jnp.sin(ang)
    return jnp.concatenate([cos, cos], axis=-1), jnp.concatenate([-sin, sin], axis=-1)


def _swap_halves(w):
    half = w.shape[-1] // 2
    return jnp.concatenate([w[..., half:], w[..., :half]], axis=-1)


def _layer(x, seq, cos2, sin2, norm_mix_g, w_in, b_gate, conv_dw, conv_dw_b, conv_ln_g, conv_ln_b, conv_w_pw,
           q_norm_g, w_q_up, kv_norm_g, w_kv_up, mla_w_o, w_out, norm_ffn_g, peer_w_q, peer_sub_keys, peer_u,
           peer_v, out_gain):
    d = x.shape[1]
    lora = q_norm_g.shape[0]
    row = lambda v: v.reshape(1, -1)
    sizes = (d, d, lora, lora, QK_ROPE, d, d)
    splits = [int(i) for i in np.cumsum(sizes)[:-1]]
    wa, wb, wq, wkv, wkr, wgc, wgm = [w.astype(BF16) for w in jnp.split(w_in, splits, axis=1)]
    u, qn, kvn, kr, sgc, sgm = _in_proj(
        x, row(norm_mix_g), (wa, wb, wq, wkv, wkr, _swap_halves(wkr), wgc, wgm), b_gate, row(q_norm_g),
        row(kv_norm_g), cos2, sin2, seq)

    ca = _conv(u, conv_dw, row(conv_dw_b), row(conv_ln_g), row(conv_ln_b), seq)

    wq_h = w_q_up.reshape(lora, MLA_HEADS, QK_NOPE + QK_ROPE).transpose(1, 0, 2).astype(BF16)
    wkv_h = w_kv_up.reshape(lora, MLA_HEADS, QK_NOPE + V_DIM).transpose(1, 0, 2).astype(BF16)
    wqr = wq_h[:, :, QK_NOPE:]
    o = _attention(qn, kvn, kr, wq_h[:, :, :QK_NOPE], wqr, _swap_halves(wqr), wkv_h[:, :, :QK_NOPE],
                   wkv_h[:, :, QK_NOPE:], cos2, sin2, seq)

    h, xn, q = _merge(x, ca, o, sgc, sgm, conv_w_pw.astype(BF16), mla_w_o.astype(BF16), w_out.astype(BF16),
                      row(norm_ffn_g), peer_w_q.astype(BF16))

    keys = peer_sub_keys.reshape(2 * PEER_HEADS, PEER_KEYS, -1).astype(BF16)
    idx, g = _route(q, keys)
    act = _peer_u(idx, xn, _pack_table(peer_u))
    return _peer_v(idx, act, g, h, row(out_gain), _pack_table(peer_v))


def kernel(x, norm_mix_g, w_in, b_gate, conv_dw, conv_dw_b, conv_ln_g, conv_ln_b, conv_w_pw, q_norm_g, w_q_up, kv_norm_g, w_kv_up, mla_w_o, w_out, norm_ffn_g, peer_w_q, peer_sub_keys, peer_u, peer_v, norm_final_g):
    b, s, d = x.shape
    depth = w_in.shape[0]
    assert depth == 1, "the last layer's PEER kernel applies the final norm; deeper stacks need a plain-gain variant"
    cos2, sin2 = _rope_tables(s)
    h = _layer(x.reshape(b * s, d), s, cos2, sin2, norm_mix_g[0], w_in[0], b_gate[0], conv_dw[0], conv_dw_b[0],
               conv_ln_g[0], conv_ln_b[0], conv_w_pw[0], q_norm_g[0], w_q_up[0], kv_norm_g[0], w_kv_up[0],
               mla_w_o[0], w_out[0], norm_ffn_g[0], peer_w_q[0], peer_sub_keys[0], peer_u[0], peer_v[0],
               norm_final_g)
    return h.reshape(b, s, d)
```
